```python
import math
import jax, jax.numpy as jnp
from jax import lax
import numpy as np

D_MODEL = 2048
BATCH = 1
SEQ = 16384
DEPTH = 1

CHUNK = 64
D_MIX = D_MODEL
SB_HEADS = 8
SB_HEAD_DIM = 128
D_SB = SB_HEADS * SB_HEAD_DIM
POOL_WINDOWS = (2, 4, 8, 16)
POOL_GROUPS = len(POOL_WINDOWS)
D_POOL = D_MIX - D_SB
POOL_GROUP = D_POOL // POOL_GROUPS
D_IN = 4 * D_SB + 2 * D_POOL
Q_BLOCK = 128
EPS = 1e-6

kernel_name = "hybrid_stickbreak_pool_block"


def _rmsnorm(x, gain):
    x32 = x.astype(jnp.float32)
    y = x32 * lax.rsqrt(jnp.mean(x32 * x32, axis=-1, keepdims=True) + EPS)
    return (y * gain.astype(jnp.float32)).astype(x.dtype)


def _stick_breaking(q, k, v):
    b, h, s, dh = q.shape
    nb = s // Q_BLOCK
    scale = 1.0 / math.sqrt(dh)
    k_pos = jnp.arange(s, dtype=jnp.int32)
    q_blocks = q.reshape(b, h, nb, Q_BLOCK, dh).transpose(2, 0, 1, 3, 4)
    starts = jnp.arange(nb, dtype=jnp.int32) * Q_BLOCK

    def block(args):
        q_blk, start = args
        q_pos = start + jnp.arange(Q_BLOCK, dtype=jnp.int32)
        z = jnp.einsum('bhqd,bhkd->bhqk', q_blk, k,
                       preferred_element_type=jnp.float32) * scale
        mask = k_pos[None, :] < q_pos[:, None]
        log_keep = jnp.where(mask, jax.nn.log_sigmoid(-z), 0.0)
        log_after = lax.cumsum(log_keep, axis=3, reverse=True) - log_keep
        a = jnp.where(mask, jnp.exp(jax.nn.log_sigmoid(z) + log_after), 0.0)
        return jnp.einsum('bhqk,bhkd->bhqd', a.astype(v.dtype), v)

    out = lax.map(block, (q_blocks, starts))
    return out.transpose(1, 2, 0, 3, 4).reshape(b, h, s, dh)


def _multiscale_pool(u, w_pool, pool_scale):
    b, s, _ = u.shape
    u32 = u.astype(jnp.float32)
    c = jnp.cumsum(u32, axis=1)
    t = jnp.arange(s, dtype=jnp.int32)
    outs = []
    for g, w in enumerate(POOL_WINDOWS):
        sl = slice(g * POOL_GROUP, (g + 1) * POOL_GROUP)
        cg = c[:, :, sl]
        lag = jnp.pad(cg[:, :s - w], ((0, 0), (w, 0), (0, 0)))
        count = jnp.minimum(t + 1, w).astype(jnp.float32)[None, :, None]
        mixed = (cg - lag) / count - u32[:, :, sl]
        outs.append(jnp.einsum('bsc,cd->bsd', mixed, w_pool[g].astype(jnp.float32)))
    y = jnp.concatenate(outs, axis=-1) * pool_scale.astype(jnp.float32)
    return y.astype(u.dtype)


def setup_inputs(seed: int = 0) -> dict:
    key = jax.random.key(seed)
    ks = jax.random.split(key, 8)
    x = jax.random.normal(ks[0], (BATCH, SEQ, D_MODEL), jnp.float32)
    pre_norm_gain = 1.0 + 0.05 * jax.random.normal(ks[1], (DEPTH, D_MODEL), jnp.float32)
    w_in = jax.random.normal(ks[2], (DEPTH, D_MODEL, D_IN), jnp.float32) * D_MODEL ** -0.5
    w_pool = jax.random.normal(ks[3], (DEPTH, POOL_GROUPS, POOL_GROUP, POOL_GROUP), jnp.float32) * POOL_GROUP ** -0.5
    pool_scale = 1.0 + 0.1 * jax.random.normal(ks[4], (DEPTH, D_POOL), jnp.float32)
    w_out = jax.random.normal(ks[5], (DEPTH, D_MIX, D_MODEL), jnp.float32) * D_MIX ** -0.5
    post_norm_gain = 1.0 + 0.05 * jax.random.normal(ks[6], (DEPTH, D_MODEL), jnp.float32)
    return {"x": x, "pre_norm_gain": pre_norm_gain, "w_in": w_in, "w_pool": w_pool,
            "pool_scale": pool_scale, "w_out": w_out, "post_norm_gain": post_norm_gain}


def reference(x, pre_norm_gain, w_in, w_pool, pool_scale, w_out, post_norm_gain):
    b, s, _ = x.shape
    split_at = [D_SB, 2 * D_SB, 3 * D_SB, 4 * D_SB, 4 * D_SB + D_POOL]
    h = x
    for layer in range(DEPTH):
        hn = _rmsnorm(h, pre_norm_gain[layer])
        proj = jnp.einsum('bsd,de->bse', hn, w_in[layer])
        q, k, v, g_a, u_b, g_b = jnp.split(proj, split_at, axis=-1)

        def heads(t):
            return t.reshape(b, s, SB_HEADS, SB_HEAD_DIM).transpose(0, 2, 1, 3)

        y_a = _stick_breaking(heads(q), heads(k), heads(v))
        y_a = y_a.transpose(0, 2, 1, 3).reshape(b, s, D_SB) * jax.nn.silu(g_a)
        y_b = _multiscale_pool(u_b, w_pool[layer], pool_scale[layer]) * jax.nn.silu(g_b)

        mix = jnp.concatenate([y_a, y_b], axis=-1)
        out = jnp.einsum('bse,ed->bsd', mix, w_out[layer])
        h = h + _rmsnorm(out, post_norm_gain[layer])
    return h
```

```python
import functools
import math

import jax
import jax.numpy as jnp
from jax import lax
from jax.experimental import pallas as pl
from jax.experimental.pallas import tpu as pltpu

D_MODEL = 2048
SB_HEADS = 8
HEAD_DIM = 128
D_SB = SB_HEADS * HEAD_DIM
POOL_WINDOWS = (2, 4, 8, 16)
POOL_GROUP = 256
D_POOL = len(POOL_WINDOWS) * POOL_GROUP
D_IN = 4 * D_SB + 2 * D_POOL
EPS = 1e-6

LANES = 128
HALO = 16
VMEM_LIMIT = 56 * 1024 * 1024

PROJ_ROWS = 1024
PROJ_COLS = 1024
ATT_Q = 128
ATT_K = 128
OUT_ROWS = 256


def _inproj_kernel(x_ref, gain_ref, w_ref, qkv_ref, gub_ref, hn_ref):
    j = pl.program_id(1)

    @pl.when(j == 0)
    def _():
        x = x_ref[...]
        ms = jnp.mean(x * x, axis=-1, keepdims=True)
        hn_ref[...] = (x * lax.rsqrt(ms + EPS) * gain_ref[...]).astype(jnp.bfloat16)

    acc = jnp.dot(hn_ref[...], w_ref[...], preferred_element_type=jnp.float32)

    @pl.when(j == 0)
    def _():
        qkv_ref[...] = (acc * (1.0 / math.sqrt(HEAD_DIM))).astype(jnp.bfloat16)

    @pl.when((j == 1) | (j == 2))
    def _():
        qkv_ref[...] = acc.astype(jnp.bfloat16)

    @pl.when(j >= 3)
    def _():
        gub_ref[...] = acc


def _inproj(x2d, gain, w_bf16):
    s = x2d.shape[0]
    grid = (s // PROJ_ROWS, D_IN // PROJ_COLS)
    return pl.pallas_call(
        _inproj_kernel,
        grid=grid,
        in_specs=[
            pl.BlockSpec((PROJ_ROWS, D_MODEL), lambda i, j: (i, 0)),
            pl.BlockSpec((1, D_MODEL), lambda i, j: (0, 0)),
            pl.BlockSpec((D_MODEL, PROJ_COLS), lambda i, j: (0, j)),
        ],
        out_specs=[
            pl.BlockSpec((PROJ_ROWS, PROJ_COLS), lambda i, j: (i, jnp.minimum(j, 2))),
            pl.BlockSpec((PROJ_ROWS, PROJ_COLS), lambda i, j: (i, jnp.maximum(j - 3, 0))),
        ],
        out_shape=[
            jax.ShapeDtypeStruct((s, 3 * D_SB), jnp.bfloat16),
            jax.ShapeDtypeStruct((s, D_SB + 2 * D_POOL), jnp.float32),
        ],
        scratch_shapes=[pltpu.VMEM((PROJ_ROWS, D_MODEL), jnp.bfloat16)],
        compiler_params=pltpu.CompilerParams(
            dimension_semantics=("arbitrary", "arbitrary"),
            vmem_limit_bytes=VMEM_LIMIT,
        ),
        name="inproj",
    )(x2d, gain, w_bf16)


def _softplus(z):
    return jnp.maximum(z, 0.0) + jnp.log(1.0 + jnp.exp(-jnp.abs(z)))


def _sb_block(q, k_blk, v_blk, tri, carry, mask):
    z = lax.dot_general(q, k_blk, (((1,), (1,)), ((), ())),
                        preferred_element_type=jnp.float32)
    sp = _softplus(z)
    if mask is not None:
        sp = jnp.where(mask, sp, 0.0)
    hi = sp.astype(jnp.bfloat16)
    lo = (sp - hi.astype(jnp.float32)).astype(jnp.bfloat16)
    r = jnp.dot(jnp.concatenate([hi, lo], axis=1), tri,
                preferred_element_type=jnp.float32)
    sfx = r[:, :ATT_K]
    tot = r[:, ATT_K:]
    a = jnp.exp(z - sfx - carry)
    if mask is not None:
        a = jnp.where(mask, a, 0.0)
    pv = jnp.dot(a.astype(jnp.bfloat16), v_blk, preferred_element_type=jnp.float32)
    return pv, carry + tot


def _attn_kernel(q_ref, k_ref, v_ref, g_ref, tri_ref, o_ref):
    i = pl.program_id(1)
    q = q_ref[...]
    tri = tri_ref[...]

    row = lax.broadcasted_iota(jnp.int32, (ATT_Q, ATT_K), 0)
    col = lax.broadcasted_iota(jnp.int32, (ATT_Q, ATT_K), 1)
    diag_start = pl.multiple_of(i * ATT_Q, ATT_Q)
    acc, carry = _sb_block(
        q, k_ref[pl.ds(diag_start, ATT_K), :], v_ref[pl.ds(diag_start, ATT_K), :], tri,
        jnp.zeros((ATT_Q, ATT_K), jnp.float32), col < row)

    def body(t, state):
        acc, carry = state
        start = pl.multiple_of((i - 1 - t) * ATT_K, ATT_K)
        pv, carry = _sb_block(q, k_ref[pl.ds(start, ATT_K), :], v_ref[pl.ds(start, ATT_K), :],
                              tri, carry, None)
        return acc + pv, carry

    acc, _ = lax.fori_loop(0, i, body, (acc, carry))
    g = g_ref[...]
    o_ref[...] = (acc * (g * jax.nn.sigmoid(g))).astype(o_ref.dtype)


def _suffix_sum_matrix():
    j = lax.broadcasted_iota(jnp.int32, (2 * ATT_K, 2 * ATT_K), 0) % ATT_K
    s = lax.broadcasted_iota(jnp.int32, (2 * ATT_K, 2 * ATT_K), 1)
    return ((s >= ATT_K) | (j >= s)).astype(jnp.bfloat16)


def _attention(qkv, gub):
    s = qkv.shape[0]
    grid = (SB_HEADS, s // ATT_Q)
    return pl.pallas_call(
        _attn_kernel,
        grid=grid,
        in_specs=[
            pl.BlockSpec((ATT_Q, HEAD_DIM), lambda h, i: (i, h)),
            pl.BlockSpec((s, HEAD_DIM), lambda h, i: (0, SB_HEADS + h)),
            pl.BlockSpec((s, HEAD_DIM), lambda h, i: (0, 2 * SB_HEADS + h)),
            pl.BlockSpec((ATT_Q, HEAD_DIM), lambda h, i: (i, h)),
            pl.BlockSpec((2 * ATT_K, 2 * ATT_K), lambda h, i: (0, 0)),
        ],
        out_specs=pl.BlockSpec((ATT_Q, HEAD_DIM), lambda h, i: (i, h)),
        out_shape=jax.ShapeDtypeStruct((s, D_SB), jnp.bfloat16),
        compiler_params=pltpu.CompilerParams(
            dimension_semantics=("arbitrary", "arbitrary"),
            vmem_limit_bytes=VMEM_LIMIT,
        ),
        name="sb_attention",
    )(qkv, qkv, qkv, gub, _suffix_sum_matrix())


def _out_kernel(ya_ref, u_ref, halo_ref, gb_ref, wp_ref, ps_ref, wo_ref, x_ref, gain_ref, o_ref):
    i = pl.program_id(0)
    halo = jnp.where(i == 0, 0.0, halo_ref[...])
    ext = jnp.concatenate([halo, u_ref[...]], axis=0)
    t = i * OUT_ROWS + lax.broadcasted_iota(jnp.int32, (OUT_ROWS, 1), 0)

    out = jnp.dot(ya_ref[...], wo_ref[:D_SB, :], preferred_element_type=jnp.float32)
    win = ext
    width = 1
    for g, w in enumerate(POOL_WINDOWS):
        while width < w:
            win = win + pltpu.roll(win, width, axis=0)
            width *= 2
        cols = slice(g * POOL_GROUP, (g + 1) * POOL_GROUP)
        count = jnp.minimum(t + 1, w).astype(jnp.float32)
        mixed = win[HALO:, cols] / count - ext[HALO:, cols]
        y = jnp.dot(mixed.astype(jnp.bfloat16), wp_ref[g], preferred_element_type=jnp.float32)
        gb = gb_ref[:, cols]
        y = y * ps_ref[:, cols] * (gb * jax.nn.sigmoid(gb))
        out = out + jnp.dot(y.astype(jnp.bfloat16), wo_ref[D_SB + g * POOL_GROUP:D_SB + (g + 1) * POOL_GROUP, :],
                            preferred_element_type=jnp.float32)

    ms = jnp.mean(out * out, axis=-1, keepdims=True)
    o_ref[...] = x_ref[...] + out * lax.rsqrt(ms + EPS) * gain_ref[...]


def _out_stage(ya, gub, wp_bf16, pool_scale, wo_bf16, x2d, gain):
    s = x2d.shape[0]
    halo_blocks = OUT_ROWS // HALO
    return pl.pallas_call(
        _out_kernel,
        grid=(s // OUT_ROWS,),
        in_specs=[
            pl.BlockSpec((OUT_ROWS, D_SB), lambda i: (i, 0)),
            pl.BlockSpec((OUT_ROWS, D_POOL), lambda i: (i, 1)),
            pl.BlockSpec((HALO, D_POOL), lambda i: (jnp.maximum(i * halo_blocks - 1, 0), 1)),
            pl.BlockSpec((OUT_ROWS, D_POOL), lambda i: (i, 2)),
            pl.BlockSpec((len(POOL_WINDOWS), POOL_GROUP, POOL_GROUP), lambda i: (0, 0, 0)),
            pl.BlockSpec((1, D_POOL), lambda i: (0, 0)),
            pl.BlockSpec((D_SB + D_POOL, D_MODEL), lambda i: (0, 0)),
            pl.BlockSpec((OUT_ROWS, D_MODEL), lambda i: (i, 0)),
            pl.BlockSpec((1, D_MODEL), lambda i: (0, 0)),
        ],
        out_specs=pl.BlockSpec((OUT_ROWS, D_MODEL), lambda i: (i, 0)),
        out_shape=jax.ShapeDtypeStruct((s, D_MODEL), jnp.float32),
        compiler_params=pltpu.CompilerParams(
            dimension_semantics=("arbitrary",),
            vmem_limit_bytes=VMEM_LIMIT,
        ),
        name="pool_outproj",
    )(ya, gub, gub, gub, wp_bf16, pool_scale, wo_bf16, x2d, gain)


def kernel(x, pre_norm_gain, w_in, w_pool, pool_scale, w_out, post_norm_gain):
    b, s, d = x.shape
    h = x.reshape(b * s, d)
    assert b == 1
    for layer in range(w_in.shape[0]):
        qkv, gub = _inproj(h, pre_norm_gain[layer][None, :], w_in[layer].astype(jnp.bfloat16))
        ya = _attention(qkv, gub)
        h = _out_stage(ya, gub, w_pool[layer].astype(jnp.bfloat16), pool_scale[layer][None, :],
                       w_out[layer].astype(jnp.bfloat16), h, post_norm_gain[layer][None, :])
    return h.reshape(b, s, d)
```

```python
import math

import jax
import jax.numpy as jnp
from jax import lax
from jax.experimental import pallas as pl
from jax.experimental.pallas import tpu as pltpu

D_MODEL = 2048
SB_HEADS = 8
HEAD_DIM = 128
D_SB = SB_HEADS * HEAD_DIM
POOL_WINDOWS = (2, 4, 8, 16)
POOL_GROUP = 256
D_POOL = len(POOL_WINDOWS) * POOL_GROUP
D_IN = 4 * D_SB + 2 * D_POOL
EPS = 1e-6

HALO = 16
VMEM_LIMIT = 56 * 1024 * 1024

PROJ_ROWS = 1024
PROJ_COLS = 1024
ATT_Q = 128
ATT_K = 128
ATT_SUB = 4
LOOKBACK = 2
SKIP_BELOW = 106.0
OUT_ROWS = 256


def _inproj_kernel(x_ref, gain_ref, w_ref, qkv_ref, gub_ref, hn_ref):
    j = pl.program_id(1)

    @pl.when(j == 0)
    def _():
        x = x_ref[...]
        ms = jnp.mean(x * x, axis=-1, keepdims=True)
        hn_ref[...] = (x * lax.rsqrt(ms + EPS) * gain_ref[...]).astype(jnp.bfloat16)

    acc = jnp.dot(hn_ref[...], w_ref[...], preferred_element_type=jnp.float32)

    @pl.when(j == 0)
    def _():
        qkv_ref[...] = (acc * (1.0 / math.sqrt(HEAD_DIM))).astype(jnp.bfloat16)

    @pl.when((j == 1) | (j == 2))
    def _():
        qkv_ref[...] = acc.astype(jnp.bfloat16)

    @pl.when(j >= 3)
    def _():
        gub_ref[...] = acc


def _inproj(x2d, gain, w_bf16):
    s = x2d.shape[0]
    grid = (s // PROJ_ROWS, D_IN // PROJ_COLS)
    return pl.pallas_call(
        _inproj_kernel,
        grid=grid,
        in_specs=[
            pl.BlockSpec((PROJ_ROWS, D_MODEL), lambda i, j: (i, 0)),
            pl.BlockSpec((1, D_MODEL), lambda i, j: (0, 0)),
            pl.BlockSpec((D_MODEL, PROJ_COLS), lambda i, j: (0, j)),
        ],
        out_specs=[
            pl.BlockSpec((PROJ_ROWS, PROJ_COLS), lambda i, j: (i, jnp.minimum(j, 2))),
            pl.BlockSpec((PROJ_ROWS, PROJ_COLS), lambda i, j: (i, jnp.maximum(j - 3, 0))),
        ],
        out_shape=[
            jax.ShapeDtypeStruct((s, 3 * D_SB), jnp.bfloat16),
            jax.ShapeDtypeStruct((s, D_SB + 2 * D_POOL), jnp.float32),
        ],
        scratch_shapes=[pltpu.VMEM((PROJ_ROWS, D_MODEL), jnp.bfloat16)],
        compiler_params=pltpu.CompilerParams(
            dimension_semantics=("arbitrary", "arbitrary"),
            vmem_limit_bytes=VMEM_LIMIT,
        ),
        name="inproj",
    )(x2d, gain, w_bf16)


def _softplus(z):
    return jnp.maximum(z, 0.0) + jnp.log(1.0 + jnp.exp(-jnp.abs(z)))


def _split_bf16(x):
    hi = x.astype(jnp.bfloat16)
    lo = (x - hi.astype(jnp.float32)).astype(jnp.bfloat16)
    return jnp.concatenate([hi, lo], axis=1)


def _scores(q, k_rows):
    return lax.dot_general(q, k_rows, (((1,), (1,)), ((), ())), preferred_element_type=jnp.float32)


def _sb_block(q, k_blk, v_blk, tri, carry, mask):
    z = _scores(q, k_blk)
    sp = _softplus(z)
    if mask is not None:
        sp = jnp.where(mask, sp, 0.0)
    r = jnp.dot(_split_bf16(sp), tri, preferred_element_type=jnp.float32)
    sfx = r[:, :ATT_K]
    tot = r[:, ATT_K:]
    a = jnp.exp(z - sfx - carry)
    if mask is not None:
        a = jnp.where(mask, a, 0.0)
    pv = jnp.dot(a.astype(jnp.bfloat16), v_blk, preferred_element_type=jnp.float32)
    return pv, carry + tot


def _window_part(q, k_win, v_win, tri, mask):
    nb = LOOKBACK + 1
    z = _scores(q, k_win)
    sp = _softplus(z)
    zs = [z[:, b * ATT_K:(b + 1) * ATT_K] for b in range(nb)]
    sps = [sp[:, b * ATT_K:(b + 1) * ATT_K] for b in range(nb)]
    sps[-1] = jnp.where(mask, sps[-1], 0.0)
    lhs = jnp.concatenate([_split_bf16(s) for s in sps], axis=0)
    r = jnp.dot(lhs, tri, preferred_element_type=jnp.float32)
    carry = jnp.zeros((ATT_Q, ATT_K), jnp.float32)
    weights = [None] * nb
    for b in reversed(range(nb)):
        rb = r[b * ATT_Q:(b + 1) * ATT_Q]
        a = jnp.exp(zs[b] - rb[:, :ATT_K] - carry)
        if b == nb - 1:
            a = jnp.where(mask, a, 0.0)
        weights[b] = a.astype(jnp.bfloat16)
        carry = carry + rb[:, ATT_K:]
    acc = jnp.dot(jnp.concatenate(weights, axis=1), v_win, preferred_element_type=jnp.float32)
    return acc, carry


def _attn_kernel(q_ref, k_ref, v_ref, g_ref, tri_ref, o_ref):
    i = pl.program_id(1)
    tri = tri_ref[...]
    row = lax.broadcasted_iota(jnp.int32, (ATT_Q, ATT_K), 0)
    col = lax.broadcasted_iota(jnp.int32, (ATT_Q, ATT_K), 1)
    mask = col < row
    first_blk = i * ATT_SUB

    def sweep_rest(q, acc, carry, j0):
        def cond(st):
            return (st[0] >= 0) & st[3]

        def body(st):
            j, acc, carry, _ = st
            start = pl.multiple_of(j * ATT_K, ATT_K)
            pv, carry = _sb_block(q, k_ref[pl.ds(start, ATT_K), :], v_ref[pl.ds(start, ATT_K), :],
                                  tri, carry, None)
            return j - 1, acc + pv, carry, jnp.min(carry) < SKIP_BELOW

        st = lax.while_loop(cond, body, (j0, acc, carry, jnp.min(carry) < SKIP_BELOW))
        return st[1]

    def finish(r, acc):
        rows = pl.ds(r * ATT_Q, ATT_Q)
        g = g_ref[rows, :]
        o_ref[rows, :] = (acc * (g * jax.nn.sigmoid(g))).astype(o_ref.dtype)

    @pl.when(i == 0)
    def _():
        for r in range(ATT_SUB):
            q = q_ref[pl.ds(r * ATT_Q, ATT_Q), :]
            acc, carry = _sb_block(q, k_ref[pl.ds(r * ATT_K, ATT_K), :], v_ref[pl.ds(r * ATT_K, ATT_K), :],
                                   tri, jnp.zeros((ATT_Q, ATT_K), jnp.float32), mask)
            finish(r, sweep_rest(q, acc, carry, r - 1))

    @pl.when(i > 0)
    def _():
        parts = []
        for r in range(ATT_SUB):
            q = q_ref[pl.ds(r * ATT_Q, ATT_Q), :]
            win = pl.ds(pl.multiple_of((first_blk + r - LOOKBACK) * ATT_K, ATT_K), (LOOKBACK + 1) * ATT_K)
            parts.append((q,) + _window_part(q, k_ref[win, :], v_ref[win, :], tri, mask))
        for r, (q, acc, carry) in enumerate(parts):
            finish(r, sweep_rest(q, acc, carry, first_blk + r - LOOKBACK - 1))


def _suffix_sum_matrix():
    j = lax.broadcasted_iota(jnp.int32, (2 * ATT_K, 2 * ATT_K), 0) % ATT_K
    s = lax.broadcasted_iota(jnp.int32, (2 * ATT_K, 2 * ATT_K), 1)
    return ((s >= ATT_K) | (j >= s)).astype(jnp.bfloat16)


def _attention(qkv, gub):
    s = qkv.shape[0]
    rows = ATT_SUB * ATT_Q
    grid = (SB_HEADS, s // rows)
    return pl.pallas_call(
        _attn_kernel,
        grid=grid,
        in_specs=[
            pl.BlockSpec((rows, HEAD_DIM), lambda h, i: (i, h)),
            pl.BlockSpec((s, HEAD_DIM), lambda h, i: (0, SB_HEADS + h)),
            pl.BlockSpec((s, HEAD_DIM), lambda h, i: (0, 2 * SB_HEADS + h)),
            pl.BlockSpec((rows, HEAD_DIM), lambda h, i: (i, h)),
            pl.BlockSpec((2 * ATT_K, 2 * ATT_K), lambda h, i: (0, 0)),
        ],
        out_specs=pl.BlockSpec((rows, HEAD_DIM), lambda h, i: (i, h)),
        out_shape=jax.ShapeDtypeStruct((s, D_SB), jnp.bfloat16),
        compiler_params=pltpu.CompilerParams(
            dimension_semantics=("arbitrary", "arbitrary"),
            vmem_limit_bytes=VMEM_LIMIT,
        ),
        name="sb_attention",
    )(qkv, qkv, qkv, gub, _suffix_sum_matrix())


def _out_kernel(ya_ref, u_ref, halo_ref, gb_ref, wp_ref, ps_ref, wo_ref, x_ref, gain_ref, o_ref):
    i = pl.program_id(0)
    halo = jnp.where(i == 0, 0.0, halo_ref[...])
    ext = jnp.concatenate([halo, u_ref[...]], axis=0)
    t = i * OUT_ROWS + lax.broadcasted_iota(jnp.int32, (OUT_ROWS, 1), 0)

    out = jnp.dot(ya_ref[...], wo_ref[:D_SB, :], preferred_element_type=jnp.float32)
    win = ext
    width = 1
    for g, w in enumerate(POOL_WINDOWS):
        while width < w:
            win = win + pltpu.roll(win, width, axis=0)
            width *= 2
        cols = slice(g * POOL_GROUP, (g + 1) * POOL_GROUP)
        count = jnp.minimum(t + 1, w).astype(jnp.float32)
        mixed = win[HALO:, cols] / count - ext[HALO:, cols]
        y = jnp.dot(mixed.astype(jnp.bfloat16), wp_ref[g], preferred_element_type=jnp.float32)
        gb = gb_ref[:, cols]
        y = y * ps_ref[:, cols] * (gb * jax.nn.sigmoid(gb))
        out = out + jnp.dot(y.astype(jnp.bfloat16), wo_ref[D_SB + g * POOL_GROUP:D_SB + (g + 1) * POOL_GROUP, :],
                            preferred_element_type=jnp.float32)

    ms = jnp.mean(out * out, axis=-1, keepdims=True)
    o_ref[...] = x_ref[...] + out * lax.rsqrt(ms + EPS) * gain_ref[...]


def _out_stage(ya, gub, wp_bf16, pool_scale, wo_bf16, x2d, gain):
    s = x2d.shape[0]
    halo_blocks = OUT_ROWS // HALO
    return pl.pallas_call(
        _out_kernel,
        grid=(s // OUT_ROWS,),
        in_specs=[
            pl.BlockSpec((OUT_ROWS, D_SB), lambda i: (i, 0)),
            pl.BlockSpec((OUT_ROWS, D_POOL), lambda i: (i, 1)),
            pl.BlockSpec((HALO, D_POOL), lambda i: (jnp.maximum(i * halo_blocks - 1, 0), 1)),
            pl.BlockSpec((OUT_ROWS, D_POOL), lambda i: (i, 2)),
            pl.BlockSpec((len(POOL_WINDOWS), POOL_GROUP, POOL_GROUP), lambda i: (0, 0, 0)),
            pl.BlockSpec((1, D_POOL), lambda i: (0, 0)),
            pl.BlockSpec((D_SB + D_POOL, D_MODEL), lambda i: (0, 0)),
            pl.BlockSpec((OUT_ROWS, D_MODEL), lambda i: (i, 0)),
            pl.BlockSpec((1, D_MODEL), lambda i: (0, 0)),
        ],
        out_specs=pl.BlockSpec((OUT_ROWS, D_MODEL), lambda i: (i, 0)),
        out_shape=jax.ShapeDtypeStruct((s, D_MODEL), jnp.float32),
        compiler_params=pltpu.CompilerParams(
            dimension_semantics=("arbitrary",),
            vmem_limit_bytes=VMEM_LIMIT,
        ),
        name="pool_outproj",
    )(ya, gub, gub, gub, wp_bf16, pool_scale, wo_bf16, x2d, gain)


def kernel(x, pre_norm_gain, w_in, w_pool, pool_scale, w_out, post_norm_gain):
    b, s, d = x.shape
    assert b == 1
    h = x.reshape(b * s, d)
    for layer in range(w_in.shape[0]):
        qkv, gub = _inproj(h, pre_norm_gain[layer][None, :], w_in[layer].astype(jnp.bfloat16))
        ya = _attention(qkv, gub)
        h = _out_stage(ya, gub, w_pool[layer].astype(jnp.bfloat16), pool_scale[layer][None, :],
                       w_out[layer].astype(jnp.bfloat16), h, post_norm_gain[layer][None, :])
    return h.reshape(b, s, d)
```

```python
import math

import jax
import jax.numpy as jnp
from jax import lax
from jax.experimental import pallas as pl
from jax.experimental.pallas import tpu as pltpu

D_MODEL = 2048
SB_HEADS = 8
HEAD_DIM = 128
D_SB = SB_HEADS * HEAD_DIM
POOL_WINDOWS = (2, 4, 8, 16)
POOL_GROUP = 256
D_POOL = len(POOL_WINDOWS) * POOL_GROUP
D_IN = 4 * D_SB + 2 * D_POOL
EPS = 1e-6

HALO = 16
VMEM_LIMIT = 56 * 1024 * 1024

PROJ_ROWS = 1024
PROJ_COLS = 1024
ATT_Q = 128
ATT_K = 128
ATT_SUB = 4
LOOKBACK = 2
SKIP_BELOW = 106.0
OUT_ROWS = 256


def _inproj_kernel(x_ref, gain_ref, w_ref, qkv_ref, gub_ref, hn_ref):
    j = pl.program_id(1)

    @pl.when(j == 0)
    def _():
        x = x_ref[...]
        ms = jnp.mean(x * x, axis=-1, keepdims=True)
        hn_ref[...] = (x * lax.rsqrt(ms + EPS) * gain_ref[...]).astype(jnp.bfloat16)

    acc = jnp.dot(hn_ref[...], w_ref[...], preferred_element_type=jnp.float32)

    @pl.when(j == 0)
    def _():
        qkv_ref[...] = (acc * (1.0 / math.sqrt(HEAD_DIM))).astype(jnp.bfloat16)

    @pl.when((j == 1) | (j == 2))
    def _():
        qkv_ref[...] = acc.astype(jnp.bfloat16)

    @pl.when(j >= 3)
    def _():
        gub_ref[...] = acc


def _inproj(x2d, gain, w_bf16):
    s = x2d.shape[0]
    grid = (s // PROJ_ROWS, D_IN // PROJ_COLS)
    return pl.pallas_call(
        _inproj_kernel,
        grid=grid,
        in_specs=[
            pl.BlockSpec((PROJ_ROWS, D_MODEL), lambda i, j: (i, 0)),
            pl.BlockSpec((1, D_MODEL), lambda i, j: (0, 0)),
            pl.BlockSpec((D_MODEL, PROJ_COLS), lambda i, j: (0, j)),
        ],
        out_specs=[
            pl.BlockSpec((PROJ_ROWS, PROJ_COLS), lambda i, j: (i, jnp.minimum(j, 2))),
            pl.BlockSpec((PROJ_ROWS, PROJ_COLS), lambda i, j: (i, jnp.maximum(j - 3, 0))),
        ],
        out_shape=[
            jax.ShapeDtypeStruct((s, 3 * D_SB), jnp.bfloat16),
            jax.ShapeDtypeStruct((s, D_SB + 2 * D_POOL), jnp.float32),
        ],
        scratch_shapes=[pltpu.VMEM((PROJ_ROWS, D_MODEL), jnp.bfloat16)],
        compiler_params=pltpu.CompilerParams(
            dimension_semantics=("arbitrary", "arbitrary"),
            vmem_limit_bytes=VMEM_LIMIT,
        ),
        name="inproj",
    )(x2d, gain, w_bf16)


def _softplus(z):
    return jnp.maximum(z, 0.0) + jnp.log(1.0 + jnp.exp(-jnp.abs(z)))


def _split_bf16(x):
    hi = x.astype(jnp.bfloat16)
    lo = (x - hi.astype(jnp.float32)).astype(jnp.bfloat16)
    return jnp.concatenate([hi, lo], axis=1)


def _scores(q, k_rows):
    return lax.dot_general(q, k_rows, (((1,), (1,)), ((), ())), preferred_element_type=jnp.float32)


def _sb_block(q, k_blk, v_blk, tri, carry, mask):
    z = _scores(q, k_blk)
    sp = _softplus(z)
    if mask is not None:
        sp = jnp.where(mask, sp, 0.0)
    r = jnp.dot(_split_bf16(sp), tri, preferred_element_type=jnp.float32)
    sfx = r[:, :ATT_K]
    tot = r[:, ATT_K:]
    a = jnp.exp(z - sfx - carry)
    if mask is not None:
        a = jnp.where(mask, a, 0.0)
    pv = jnp.dot(a.astype(jnp.bfloat16), v_blk, preferred_element_type=jnp.float32)
    return pv, carry + tot


def _window_parts(qs, k_wins, v_wins, tri, mask):
    nb = LOOKBACK + 1
    n = len(qs)
    zs = [_scores(q, k) for q, k in zip(qs, k_wins)]
    sps = []
    for z in zs:
        sp = _softplus(z)
        blocks = [sp[:, b * ATT_K:(b + 1) * ATT_K] for b in range(nb)]
        blocks[-1] = jnp.where(mask, blocks[-1], 0.0)
        sps.append(blocks)
    lhs = jnp.concatenate([_split_bf16(s) for blocks in sps for s in blocks], axis=0)
    r_all = jnp.dot(lhs, tri, preferred_element_type=jnp.float32)
    weights, carries = [], []
    for c in range(n):
        carry = jnp.zeros((ATT_Q, ATT_K), jnp.float32)
        w = [None] * nb
        for b in reversed(range(nb)):
            rb = r_all[(c * nb + b) * ATT_Q:(c * nb + b + 1) * ATT_Q]
            a = jnp.exp(zs[c][:, b * ATT_K:(b + 1) * ATT_K] - rb[:, :ATT_K] - carry)
            if b == nb - 1:
                a = jnp.where(mask, a, 0.0)
            w[b] = a.astype(jnp.bfloat16)
            carry = carry + rb[:, ATT_K:]
        weights.append(jnp.concatenate(w, axis=1))
        carries.append(carry)
    accs = [jnp.dot(w, v, preferred_element_type=jnp.float32) for w, v in zip(weights, v_wins)]
    return accs, carries


def _attn_kernel(q_ref, k_ref, v_ref, g_ref, tri_ref, o_ref, acc_ref, carry_ref):
    i = pl.program_id(1)
    tri = tri_ref[...]
    row = lax.broadcasted_iota(jnp.int32, (ATT_Q, ATT_K), 0)
    col = lax.broadcasted_iota(jnp.int32, (ATT_Q, ATT_K), 1)
    mask = col < row
    first_blk = i * ATT_SUB

    def sweep_rest(r, j0):
        q = q_ref[pl.ds(r * ATT_Q, ATT_Q), :]

        def cond(st):
            return (st[0] >= 0) & st[3]

        def body(st):
            j, acc, carry, _ = st
            start = pl.multiple_of(j * ATT_K, ATT_K)
            pv, carry = _sb_block(q, k_ref[pl.ds(start, ATT_K), :], v_ref[pl.ds(start, ATT_K), :],
                                  tri, carry, None)
            return j - 1, acc + pv, carry, jnp.min(carry) < SKIP_BELOW

        carry = carry_ref[r]
        st = lax.while_loop(cond, body, (j0, acc_ref[r], carry, jnp.min(carry) < SKIP_BELOW))
        acc_ref[r] = st[1]

    @pl.when(i == 0)
    def _():
        for r in range(ATT_SUB):
            q = q_ref[pl.ds(r * ATT_Q, ATT_Q), :]
            acc_ref[r], carry_ref[r] = _sb_block(
                q, k_ref[pl.ds(r * ATT_K, ATT_K), :], v_ref[pl.ds(r * ATT_K, ATT_K), :],
                tri, jnp.zeros((ATT_Q, ATT_K), jnp.float32), mask)
            sweep_rest(r, r - 1)

    @pl.when(i > 0)
    def _():
        qs = [q_ref[pl.ds(r * ATT_Q, ATT_Q), :] for r in range(ATT_SUB)]
        wins = [pl.ds(pl.multiple_of((first_blk + r - LOOKBACK) * ATT_K, ATT_K), (LOOKBACK + 1) * ATT_K)
                for r in range(ATT_SUB)]
        accs, carries = _window_parts(qs, [k_ref[w, :] for w in wins], [v_ref[w, :] for w in wins], tri, mask)
        least = carries[0]
        for r in range(ATT_SUB):
            acc_ref[r] = accs[r]
            carry_ref[r] = carries[r]
            least = jnp.minimum(least, carries[r])

        @pl.when(jnp.min(least) < SKIP_BELOW)
        def _():
            for r in range(ATT_SUB):
                sweep_rest(r, first_blk + r - LOOKBACK - 1)

    g = g_ref[...]
    acc = acc_ref[...].reshape(ATT_SUB * ATT_Q, HEAD_DIM)
    o_ref[...] = (acc * (g * jax.nn.sigmoid(g))).astype(o_ref.dtype)


def _suffix_sum_matrix():
    j = lax.broadcasted_iota(jnp.int32, (2 * ATT_K, 2 * ATT_K), 0) % ATT_K
    s = lax.broadcasted_iota(jnp.int32, (2 * ATT_K, 2 * ATT_K), 1)
    return ((s >= ATT_K) | (j >= s)).astype(jnp.bfloat16)


def _attention(qkv, gub):
    s = qkv.shape[0]
    rows = ATT_SUB * ATT_Q
    grid = (SB_HEADS, s // rows)
    return pl.pallas_call(
        _attn_kernel,
        grid=grid,
        in_specs=[
            pl.BlockSpec((rows, HEAD_DIM), lambda h, i: (i, h)),
            pl.BlockSpec((s, HEAD_DIM), lambda h, i: (0, SB_HEADS + h)),
            pl.BlockSpec((s, HEAD_DIM), lambda h, i: (0, 2 * SB_HEADS + h)),
            pl.BlockSpec((rows, HEAD_DIM), lambda h, i: (i, h)),
            pl.BlockSpec((2 * ATT_K, 2 * ATT_K), lambda h, i: (0, 0)),
        ],
        out_specs=pl.BlockSpec((rows, HEAD_DIM), lambda h, i: (i, h)),
        out_shape=jax.ShapeDtypeStruct((s, D_SB), jnp.bfloat16),
        scratch_shapes=[pltpu.VMEM((ATT_SUB, ATT_Q, HEAD_DIM), jnp.float32),
                        pltpu.VMEM((ATT_SUB, ATT_Q, ATT_K), jnp.float32)],
        compiler_params=pltpu.CompilerParams(
            dimension_semantics=("arbitrary", "arbitrary"),
            vmem_limit_bytes=VMEM_LIMIT,
        ),
        name="sb_attention",
    )(qkv, qkv, qkv, gub, _suffix_sum_matrix())


def _out_kernel(ya_ref, u_ref, halo_ref, gb_ref, wp_ref, ps_ref, wo_ref, x_ref, gain_ref, o_ref):
    i = pl.program_id(0)
    halo = jnp.where(i == 0, 0.0, halo_ref[...])
    ext = jnp.concatenate([halo, u_ref[...]], axis=0)
    t = i * OUT_ROWS + lax.broadcasted_iota(jnp.int32, (OUT_ROWS, 1), 0)

    out = jnp.dot(ya_ref[...], wo_ref[:D_SB, :], preferred_element_type=jnp.float32)
    win = ext
    width = 1
    for g, w in enumerate(POOL_WINDOWS):
        while width < w:
            win = win + pltpu.roll(win, width, axis=0)
            width *= 2
        cols = slice(g * POOL_GROUP, (g + 1) * POOL_GROUP)
        count = jnp.minimum(t + 1, w).astype(jnp.float32)
        mixed = win[HALO:, cols] / count - ext[HALO:, cols]
        y = jnp.dot(mixed.astype(jnp.bfloat16), wp_ref[g], preferred_element_type=jnp.float32)
        gb = gb_ref[:, cols]
        y = y * ps_ref[:, cols] * (gb * jax.nn.sigmoid(gb))
        out = out + jnp.dot(y.astype(jnp.bfloat16), wo_ref[D_SB + g * POOL_GROUP:D_SB + (g + 1) * POOL_GROUP, :],
                            preferred_element_type=jnp.float32)

    ms = jnp.mean(out * out, axis=-1, keepdims=True)
    o_ref[...] = x_ref[...] + out * lax.rsqrt(ms + EPS) * gain_ref[...]


def _out_stage(ya, gub, wp_bf16, pool_scale, wo_bf16, x2d, gain):
    s = x2d.shape[0]
    halo_blocks = OUT_ROWS // HALO
    return pl.pallas_call(
        _out_kernel,
        grid=(s // OUT_ROWS,),
        in_specs=[
            pl.BlockSpec((OUT_ROWS, D_SB), lambda i: (i, 0)),
            pl.BlockSpec((OUT_ROWS, D_POOL), lambda i: (i, 1)),
            pl.BlockSpec((HALO, D_POOL), lambda i: (jnp.maximum(i * halo_blocks - 1, 0), 1)),
            pl.BlockSpec((OUT_ROWS, D_POOL), lambda i: (i, 2)),
            pl.BlockSpec((len(POOL_WINDOWS), POOL_GROUP, POOL_GROUP), lambda i: (0, 0, 0)),
            pl.BlockSpec((1, D_POOL), lambda i: (0, 0)),
            pl.BlockSpec((D_SB + D_POOL, D_MODEL), lambda i: (0, 0)),
            pl.BlockSpec((OUT_ROWS, D_MODEL), lambda i: (i, 0)),
            pl.BlockSpec((1, D_MODEL), lambda i: (0, 0)),
        ],
        out_specs=pl.BlockSpec((OUT_ROWS, D_MODEL), lambda i: (i, 0)),
        out_shape=jax.ShapeDtypeStruct((s, D_MODEL), jnp.float32),
        compiler_params=pltpu.CompilerParams(
            dimension_semantics=("arbitrary",),
            vmem_limit_bytes=VMEM_LIMIT,
        ),
        name="pool_outproj",
    )(ya, gub, gub, gub, wp_bf16, pool_scale, wo_bf16, x2d, gain)


def kernel(x, pre_norm_gain, w_in, w_pool, pool_scale, w_out, post_norm_gain):
    b, s, d = x.shape
    assert b == 1
    h = x.reshape(b * s, d)
    for layer in range(w_in.shape[0]):
        qkv, gub = _inproj(h, pre_norm_gain[layer][None, :], w_in[layer].astype(jnp.bfloat16))
        ya = _attention(qkv, gub)
        h = _out_stage(ya, gub, w_pool[layer].astype(jnp.bfloat16), pool_scale[layer][None, :],
                       w_out[layer].astype(jnp.bfloat16), h, post_norm_gain[layer][None, :])
    return h.reshape(b, s, d)
```

```python
import math

import jax
import jax.numpy as jnp
from jax import lax
from jax.experimental import pallas as pl
from jax.experimental.pallas import tpu as pltpu

D_MODEL = 2048
SB_HEADS = 8
HEAD_DIM = 128
D_SB = SB_HEADS * HEAD_DIM
POOL_WINDOWS = (2, 4, 8, 16)
POOL_GROUP = 256
D_POOL = len(POOL_WINDOWS) * POOL_GROUP
D_IN = 4 * D_SB + 2 * D_POOL
EPS = 1e-6

HALO = 16
VMEM_LIMIT = 56 * 1024 * 1024

PROJ_ROWS = 1024
PROJ_COLS = 1024
ATT_Q = 128
ATT_K = 128
ATT_SUB = 8
LOOKBACK = 2
SKIP_BELOW = 106.0
OUT_ROWS = 256


def _inproj_kernel(x_ref, gain_ref, wa_ref, wb_ref, qkv_ref, gub_ref, hn_ref):
    j = pl.program_id(1)

    @pl.when(j == 0)
    def _():
        x = x_ref[...]
        ms = jnp.mean(x * x, axis=-1, keepdims=True)
        hn_ref[...] = (x * lax.rsqrt(ms + EPS) * gain_ref[...]).astype(jnp.bfloat16)

    hn = hn_ref[...]
    scale = jnp.where(j == 0, 1.0 / math.sqrt(HEAD_DIM), 1.0).astype(jnp.float32)
    qkv_ref[...] = (jnp.dot(hn, wa_ref[...], preferred_element_type=jnp.float32) * scale).astype(jnp.bfloat16)
    gub_ref[...] = jnp.dot(hn, wb_ref[...], preferred_element_type=jnp.float32)


def _inproj(x2d, gain, w_bf16):
    s = x2d.shape[0]
    half_tiles = (3 * D_SB) // PROJ_COLS
    grid = (s // PROJ_ROWS, half_tiles)
    return pl.pallas_call(
        _inproj_kernel,
        grid=grid,
        in_specs=[
            pl.BlockSpec((PROJ_ROWS, D_MODEL), lambda i, j: (i, 0)),
            pl.BlockSpec((1, D_MODEL), lambda i, j: (0, 0)),
            pl.BlockSpec((D_MODEL, PROJ_COLS), lambda i, j: (0, j)),
            pl.BlockSpec((D_MODEL, PROJ_COLS), lambda i, j: (0, half_tiles + j)),
        ],
        out_specs=[
            pl.BlockSpec((PROJ_ROWS, PROJ_COLS), lambda i, j: (i, j)),
            pl.BlockSpec((PROJ_ROWS, PROJ_COLS), lambda i, j: (i, j)),
        ],
        out_shape=[
            jax.ShapeDtypeStruct((s, 3 * D_SB), jnp.bfloat16),
            jax.ShapeDtypeStruct((s, D_SB + 2 * D_POOL), jnp.float32),
        ],
        scratch_shapes=[pltpu.VMEM((PROJ_ROWS, D_MODEL), jnp.bfloat16)],
        compiler_params=pltpu.CompilerParams(
            dimension_semantics=("arbitrary", "arbitrary"),
            vmem_limit_bytes=VMEM_LIMIT,
        ),
        name="inproj",
    )(x2d, gain, w_bf16, w_bf16)


def _softplus(z):
    return jnp.maximum(z, 0.0) + jnp.log(1.0 + jnp.exp(-jnp.abs(z)))


def _split_bf16(x):
    hi = x.astype(jnp.bfloat16)
    lo = (x - hi.astype(jnp.float32)).astype(jnp.bfloat16)
    return jnp.concatenate([hi, lo], axis=1)


def _scores(q, k_rows):
    return lax.dot_general(q, k_rows, (((1,), (1,)), ((), ())), preferred_element_type=jnp.float32)


def _sb_block(q, k_blk, v_blk, tri, carry, mask):
    z = _scores(q, k_blk)
    sp = _softplus(z)
    if mask is not None:
        sp = jnp.where(mask, sp, 0.0)
    r = jnp.dot(_split_bf16(sp), tri, preferred_element_type=jnp.float32)
    sfx = r[:, :ATT_K]
    tot = r[:, ATT_K:]
    a = jnp.exp(z - sfx - carry)
    if mask is not None:
        a = jnp.where(mask, a, 0.0)
    pv = jnp.dot(a.astype(jnp.bfloat16), v_blk, preferred_element_type=jnp.float32)
    return pv, carry + tot


def _window_parts(qs, k_wins, v_wins, tri, mask):
    nbs = [k.shape[0] // ATT_K for k in k_wins]
    zs = [_scores(q, k) for q, k in zip(qs, k_wins)]
    sps = []
    for z, nb in zip(zs, nbs):
        sp = _softplus(z)
        blocks = [sp[:, b * ATT_K:(b + 1) * ATT_K] for b in range(nb)]
        blocks[-1] = jnp.where(mask, blocks[-1], 0.0)
        sps.append(blocks)
    lhs = jnp.concatenate([_split_bf16(s) for blocks in sps for s in blocks], axis=0)
    r_all = jnp.dot(lhs, tri, preferred_element_type=jnp.float32)
    weights, carries = [], []
    base = 0
    for z, nb in zip(zs, nbs):
        carry = jnp.zeros((ATT_Q, ATT_K), jnp.float32)
        w = [None] * nb
        for b in reversed(range(nb)):
            rb = r_all[(base + b) * ATT_Q:(base + b + 1) * ATT_Q]
            a = jnp.exp(z[:, b * ATT_K:(b + 1) * ATT_K] - rb[:, :ATT_K] - carry)
            if b == nb - 1:
                a = jnp.where(mask, a, 0.0)
            w[b] = a.astype(jnp.bfloat16)
            carry = carry + rb[:, ATT_K:]
        weights.append(w[0] if nb == 1 else jnp.concatenate(w, axis=1))
        carries.append(carry)
        base += nb
    accs = [jnp.dot(w, v, preferred_element_type=jnp.float32) for w, v in zip(weights, v_wins)]
    return accs, carries


def _attn_kernel(q_ref, k_ref, v_ref, g_ref, tri_ref, o_ref, acc_ref, carry_ref):
    i = pl.program_id(1)
    tri = tri_ref[...]
    row = lax.broadcasted_iota(jnp.int32, (ATT_Q, ATT_K), 0)
    col = lax.broadcasted_iota(jnp.int32, (ATT_Q, ATT_K), 1)
    mask = col < row

    def sweep_rest(r, j0):
        q = q_ref[pl.ds(r * ATT_Q, ATT_Q), :]

        def cond(st):
            return (st[0] >= 0) & st[3]

        def body(st):
            j, acc, carry, _ = st
            start = pl.multiple_of(j * ATT_K, ATT_K)
            pv, carry = _sb_block(q, k_ref[pl.ds(start, ATT_K), :], v_ref[pl.ds(start, ATT_K), :],
                                  tri, carry, None)
            return j - 1, acc + pv, carry, jnp.min(carry) < SKIP_BELOW

        carry = carry_ref[r]
        st = lax.while_loop(cond, body, (j0, acc_ref[r], carry, jnp.min(carry) < SKIP_BELOW))
        acc_ref[r] = st[1]

    def step(first_blk, lookbacks):
        qs = [q_ref[pl.ds(r * ATT_Q, ATT_Q), :] for r in range(ATT_SUB)]
        wins = [pl.ds(pl.multiple_of((first_blk + r - lb) * ATT_K, ATT_K), (lb + 1) * ATT_K)
                for r, lb in enumerate(lookbacks)]
        accs, carries = _window_parts(qs, [k_ref[w, :] for w in wins], [v_ref[w, :] for w in wins], tri, mask)
        least = None
        for r, lb in enumerate(lookbacks):
            acc_ref[r] = accs[r]
            carry_ref[r] = carries[r]
            if isinstance(first_blk, int) and first_blk + r - lb == 0:
                continue
            least = carries[r] if least is None else jnp.minimum(least, carries[r])

        @pl.when(jnp.min(least) < SKIP_BELOW)
        def _():
            for r, lb in enumerate(lookbacks):
                sweep_rest(r, first_blk + r - lb - 1)

    @pl.when(i == 0)
    def _():
        step(0, [min(r, LOOKBACK) for r in range(ATT_SUB)])

    @pl.when(i > 0)
    def _():
        step(i * ATT_SUB, [LOOKBACK] * ATT_SUB)

    g = g_ref[...]
    acc = acc_ref[...].reshape(ATT_SUB * ATT_Q, HEAD_DIM)
    o_ref[...] = (acc * (g * jax.nn.sigmoid(g))).astype(o_ref.dtype)


def _suffix_sum_matrix():
    j = lax.broadcasted_iota(jnp.int32, (2 * ATT_K, 2 * ATT_K), 0) % ATT_K
    s = lax.broadcasted_iota(jnp.int32, (2 * ATT_K, 2 * ATT_K), 1)
    return ((s >= ATT_K) | (j >= s)).astype(jnp.bfloat16)


def _attention(qkv, gub):
    s = qkv.shape[0]
    rows = ATT_SUB * ATT_Q
    grid = (SB_HEADS, s // rows)
    return pl.pallas_call(
        _attn_kernel,
        grid=grid,
        in_specs=[
            pl.BlockSpec((rows, HEAD_DIM), lambda h, i: (i, h)),
            pl.BlockSpec((s, HEAD_DIM), lambda h, i: (0, SB_HEADS + h)),
            pl.BlockSpec((s, HEAD_DIM), lambda h, i: (0, 2 * SB_HEADS + h)),
            pl.BlockSpec((rows, HEAD_DIM), lambda h, i: (i, h)),
            pl.BlockSpec((2 * ATT_K, 2 * ATT_K), lambda h, i: (0, 0)),
        ],
        out_specs=pl.BlockSpec((rows, HEAD_DIM), lambda h, i: (i, h)),
        out_shape=jax.ShapeDtypeStruct((s, D_SB), jnp.bfloat16),
        scratch_shapes=[pltpu.VMEM((ATT_SUB, ATT_Q, HEAD_DIM), jnp.float32),
                        pltpu.VMEM((ATT_SUB, ATT_Q, ATT_K), jnp.float32)],
        compiler_params=pltpu.CompilerParams(
            dimension_semantics=("arbitrary", "arbitrary"),
            vmem_limit_bytes=VMEM_LIMIT,
        ),
        name="sb_attention",
    )(qkv, qkv, qkv, gub, _suffix_sum_matrix())


def _out_kernel(ya_ref, u_ref, halo_ref, gb_ref, wp_ref, ps_ref, wo_ref, x_ref, gain_ref, o_ref):
    i = pl.program_id(0)
    halo = jnp.where(i == 0, 0.0, halo_ref[...])
    ext = jnp.concatenate([halo, u_ref[...]], axis=0)
    t = i * OUT_ROWS + lax.broadcasted_iota(jnp.int32, (OUT_ROWS, 1), 0)

    out = jnp.dot(ya_ref[...], wo_ref[:D_SB, :], preferred_element_type=jnp.float32)
    win = ext
    width = 1
    for g, w in enumerate(POOL_WINDOWS):
        while width < w:
            win = win + pltpu.roll(win, width, axis=0)
            width *= 2
        cols = slice(g * POOL_GROUP, (g + 1) * POOL_GROUP)
        count = jnp.minimum(t + 1, w).astype(jnp.float32)
        mixed = win[HALO:, cols] / count - ext[HALO:, cols]
        y = jnp.dot(mixed.astype(jnp.bfloat16), wp_ref[g], preferred_element_type=jnp.float32)
        gb = gb_ref[:, cols]
        y = y * ps_ref[:, cols] * (gb * jax.nn.sigmoid(gb))
        out = out + jnp.dot(y.astype(jnp.bfloat16), wo_ref[D_SB + g * POOL_GROUP:D_SB + (g + 1) * POOL_GROUP, :],
                            preferred_element_type=jnp.float32)

    ms = jnp.mean(out * out, axis=-1, keepdims=True)
    o_ref[...] = x_ref[...] + out * lax.rsqrt(ms + EPS) * gain_ref[...]


def _out_stage(ya, gub, wp_bf16, pool_scale, wo_bf16, x2d, gain):
    s = x2d.shape[0]
    halo_blocks = OUT_ROWS // HALO
    return pl.pallas_call(
        _out_kernel,
        grid=(s // OUT_ROWS,),
        in_specs=[
            pl.BlockSpec((OUT_ROWS, D_SB), lambda i: (i, 0)),
            pl.BlockSpec((OUT_ROWS, D_POOL), lambda i: (i, 1)),
            pl.BlockSpec((HALO, D_POOL), lambda i: (jnp.maximum(i * halo_blocks - 1, 0), 1)),
            pl.BlockSpec((OUT_ROWS, D_POOL), lambda i: (i, 2)),
            pl.BlockSpec((len(POOL_WINDOWS), POOL_GROUP, POOL_GROUP), lambda i: (0, 0, 0)),
            pl.BlockSpec((1, D_POOL), lambda i: (0, 0)),
            pl.BlockSpec((D_SB + D_POOL, D_MODEL), lambda i: (0, 0)),
            pl.BlockSpec((OUT_ROWS, D_MODEL), lambda i: (i, 0)),
            pl.BlockSpec((1, D_MODEL), lambda i: (0, 0)),
        ],
        out_specs=pl.BlockSpec((OUT_ROWS, D_MODEL), lambda i: (i, 0)),
        out_shape=jax.ShapeDtypeStruct((s, D_MODEL), jnp.float32),
        compiler_params=pltpu.CompilerParams(
            dimension_semantics=("arbitrary",),
            vmem_limit_bytes=VMEM_LIMIT,
        ),
        name="pool_outproj",
    )(ya, gub, gub, gub, wp_bf16, pool_scale, wo_bf16, x2d, gain)


def kernel(x, pre_norm_gain, w_in, w_pool, pool_scale, w_out, post_norm_gain):
    b, s, d = x.shape
    assert b == 1
    h = x.reshape(b * s, d)
    for layer in range(w_in.shape[0]):
        qkv, gub = _inproj(h, pre_norm_gain[layer][None, :], w_in[layer].astype(jnp.bfloat16))
        ya = _attention(qkv, gub)
        h = _out_stage(ya, gub, w_pool[layer].astype(jnp.bfloat16), pool_scale[layer][None, :],
                       w_out[layer].astype(jnp.bfloat16), h, post_norm_gain[layer][None, :])
    return h.reshape(b, s, d)
```

```python
import math

import jax
import jax.numpy as jnp
from jax import lax
from jax.experimental import pallas as pl
from jax.experimental.pallas import tpu as pltpu

D_MODEL = 2048
SB_HEADS = 8
HEAD_DIM = 128
D_SB = SB_HEADS * HEAD_DIM
POOL_WINDOWS = (2, 4, 8, 16)
POOL_GROUP = 256
D_POOL = len(POOL_WINDOWS) * POOL_GROUP
D_IN = 4 * D_SB + 2 * D_POOL
EPS = 1e-6

HALO = 16
VMEM_LIMIT = 56 * 1024 * 1024

PROJ_ROWS = 1024
PROJ_COLS = 1024
ATT_Q = 128
ATT_K = 128
ATT_SUB = 16
LOOKBACK = 2
LOG2E = 1.4426950408889634
SKIP_BELOW = 152.0
MASKED = -1e30
EXP2_CLAMP = 126.0
OUT_ROWS = 256


def _inproj_kernel(x_ref, gain_ref, wa_ref, wb_ref, qkv_ref, gub_ref, hn_ref):
    j = pl.program_id(1)

    @pl.when(j == 0)
    def _():
        x = x_ref[...]
        ms = jnp.mean(x * x, axis=-1, keepdims=True)
        hn_ref[...] = (x * lax.rsqrt(ms + EPS) * gain_ref[...]).astype(jnp.bfloat16)

    hn = hn_ref[...]
    scale = jnp.where(j == 0, LOG2E / math.sqrt(HEAD_DIM), 1.0).astype(jnp.float32)
    qkv_ref[...] = (jnp.dot(hn, wa_ref[...], preferred_element_type=jnp.float32) * scale).astype(jnp.bfloat16)
    gub_ref[...] = jnp.dot(hn, wb_ref[...], preferred_element_type=jnp.float32)


def _inproj(x2d, gain, w_bf16):
    s = x2d.shape[0]
    half_tiles = (3 * D_SB) // PROJ_COLS
    grid = (s // PROJ_ROWS, half_tiles)
    return pl.pallas_call(
        _inproj_kernel,
        grid=grid,
        in_specs=[
            pl.BlockSpec((PROJ_ROWS, D_MODEL), lambda i, j: (i, 0)),
            pl.BlockSpec((1, D_MODEL), lambda i, j: (0, 0)),
            pl.BlockSpec((D_MODEL, PROJ_COLS), lambda i, j: (0, j)),
            pl.BlockSpec((D_MODEL, PROJ_COLS), lambda i, j: (0, half_tiles + j)),
        ],
        out_specs=[
            pl.BlockSpec((PROJ_ROWS, PROJ_COLS), lambda i, j: (i, j)),
            pl.BlockSpec((PROJ_ROWS, PROJ_COLS), lambda i, j: (i, j)),
        ],
        out_shape=[
            jax.ShapeDtypeStruct((s, 3 * D_SB), jnp.bfloat16),
            jax.ShapeDtypeStruct((s, D_SB + 2 * D_POOL), jnp.float32),
        ],
        scratch_shapes=[pltpu.VMEM((PROJ_ROWS, D_MODEL), jnp.bfloat16)],
        compiler_params=pltpu.CompilerParams(
            dimension_semantics=("arbitrary", "arbitrary"),
            vmem_limit_bytes=VMEM_LIMIT,
        ),
        name="inproj",
    )(x2d, gain, w_bf16, w_bf16)


def _softplus2(z):
    return jnp.maximum(jnp.log(1.0 + jnp.exp2(jnp.minimum(z, EXP2_CLAMP))) * LOG2E, z)


def _split_bf16(x):
    hi = x.astype(jnp.bfloat16)
    lo = (x - hi.astype(jnp.float32)).astype(jnp.bfloat16)
    return jnp.concatenate([hi, lo], axis=1)


def _scores(q, k_rows):
    return lax.dot_general(q, k_rows, (((1,), (1,)), ((), ())), preferred_element_type=jnp.float32)


def _sb_block(q, k_blk, v_blk, tri, carry):
    z = _scores(q, k_blk)
    sp = _softplus2(z)
    r = jnp.dot(_split_bf16(sp), tri, preferred_element_type=jnp.float32)
    sfx = r[:, :ATT_K]
    tot = r[:, ATT_K:]
    a = jnp.exp2(z - sfx - carry)
    pv = jnp.dot(a.astype(jnp.bfloat16), v_blk, preferred_element_type=jnp.float32)
    return pv, carry + tot


def _window_parts(qs, k_wins, v_wins, tri, mask):
    nbs = [k.shape[0] // ATT_K for k in k_wins]
    zs = []
    for q, k, nb in zip(qs, k_wins, nbs):
        z = _scores(q, k)
        blocks = [z[:, b * ATT_K:(b + 1) * ATT_K] for b in range(nb)]
        blocks[-1] = jnp.where(mask, blocks[-1], MASKED)
        zs.append(blocks)
    lhs = jnp.concatenate([_split_bf16(_softplus2(z)) for blocks in zs for z in blocks], axis=0)
    r_all = jnp.dot(lhs, tri, preferred_element_type=jnp.float32)
    weights, carries = [], []
    base = 0
    for blocks, nb in zip(zs, nbs):
        w = [None] * nb
        carry = None
        for b in reversed(range(nb)):
            rb = r_all[(base + b) * ATT_Q:(base + b + 1) * ATT_Q]
            arg = blocks[b] - rb[:, :ATT_K]
            w[b] = jnp.exp2(arg if carry is None else arg - carry).astype(jnp.bfloat16)
            carry = rb[:, ATT_K:] if carry is None else carry + rb[:, ATT_K:]
        weights.append(w[0] if nb == 1 else jnp.concatenate(w, axis=1))
        carries.append(carry)
        base += nb
    accs = [jnp.dot(w, v, preferred_element_type=jnp.float32) for w, v in zip(weights, v_wins)]
    return accs, carries


def _attn_kernel(q_ref, k_ref, v_ref, g_ref, tri_ref, o_ref, acc_ref, carry_ref):
    i = pl.program_id(1)
    tri = tri_ref[...]
    row = lax.broadcasted_iota(jnp.int32, (ATT_Q, ATT_K), 0)
    col = lax.broadcasted_iota(jnp.int32, (ATT_Q, ATT_K), 1)
    mask = col < row

    def sweep_rest(r, j0):
        q = q_ref[pl.ds(r * ATT_Q, ATT_Q), :]

        def cond(st):
            return (st[0] >= 0) & st[3]

        def body(st):
            j, acc, carry, _ = st
            start = pl.multiple_of(j * ATT_K, ATT_K)
            pv, carry = _sb_block(q, k_ref[pl.ds(start, ATT_K), :], v_ref[pl.ds(start, ATT_K), :],
                                  tri, carry)
            return j - 1, acc + pv, carry, jnp.min(carry) < SKIP_BELOW

        carry = carry_ref[r]
        st = lax.while_loop(cond, body, (j0, acc_ref[r], carry, jnp.min(carry) < SKIP_BELOW))
        acc_ref[r] = st[1]

    def step(first_blk, lookbacks):
        qs = [q_ref[pl.ds(r * ATT_Q, ATT_Q), :] for r in range(ATT_SUB)]
        wins = [pl.ds(pl.multiple_of((first_blk + r - lb) * ATT_K, ATT_K), (lb + 1) * ATT_K)
                for r, lb in enumerate(lookbacks)]
        accs, carries = _window_parts(qs, [k_ref[w, :] for w in wins], [v_ref[w, :] for w in wins], tri, mask)
        least = None
        for r, lb in enumerate(lookbacks):
            acc_ref[r] = accs[r]
            carry_ref[r] = carries[r]
            if isinstance(first_blk, int) and first_blk + r - lb == 0:
                continue
            least = carries[r] if least is None else jnp.minimum(least, carries[r])

        @pl.when(jnp.min(least) < SKIP_BELOW)
        def _():
            for r, lb in enumerate(lookbacks):
                sweep_rest(r, first_blk + r - lb - 1)

    @pl.when(i == 0)
    def _():
        step(0, [min(r, LOOKBACK) for r in range(ATT_SUB)])

    @pl.when(i > 0)
    def _():
        step(i * ATT_SUB, [LOOKBACK] * ATT_SUB)

    g = g_ref[...]
    acc = acc_ref[...].reshape(ATT_SUB * ATT_Q, HEAD_DIM)
    o_ref[...] = (acc * (g * jax.nn.sigmoid(g))).astype(o_ref.dtype)


def _suffix_sum_matrix():
    j = lax.broadcasted_iota(jnp.int32, (2 * ATT_K, 2 * ATT_K), 0) % ATT_K
    s = lax.broadcasted_iota(jnp.int32, (2 * ATT_K, 2 * ATT_K), 1)
    return ((s >= ATT_K) | (j >= s)).astype(jnp.bfloat16)


def _attention(qkv, gub):
    s = qkv.shape[0]
    rows = ATT_SUB * ATT_Q
    grid = (SB_HEADS, s // rows)
    return pl.pallas_call(
        _attn_kernel,
        grid=grid,
        in_specs=[
            pl.BlockSpec((rows, HEAD_DIM), lambda h, i: (i, h)),
            pl.BlockSpec((s, HEAD_DIM), lambda h, i: (0, SB_HEADS + h)),
            pl.BlockSpec((s, HEAD_DIM), lambda h, i: (0, 2 * SB_HEADS + h)),
            pl.BlockSpec((rows, HEAD_DIM), lambda h, i: (i, h)),
            pl.BlockSpec((2 * ATT_K, 2 * ATT_K), lambda h, i: (0, 0)),
        ],
        out_specs=pl.BlockSpec((rows, HEAD_DIM), lambda h, i: (i, h)),
        out_shape=jax.ShapeDtypeStruct((s, D_SB), jnp.bfloat16),
        scratch_shapes=[pltpu.VMEM((ATT_SUB, ATT_Q, HEAD_DIM), jnp.float32),
                        pltpu.VMEM((ATT_SUB, ATT_Q, ATT_K), jnp.float32)],
        compiler_params=pltpu.CompilerParams(
            dimension_semantics=("arbitrary", "arbitrary"),
            vmem_limit_bytes=VMEM_LIMIT,
        ),
        name="sb_attention",
    )(qkv, qkv, qkv, gub, _suffix_sum_matrix())


def _out_kernel(ya_ref, u_ref, halo_ref, gb_ref, wp_ref, ps_ref, wo_ref, x_ref, gain_ref, o_ref):
    i = pl.program_id(0)
    halo = jnp.where(i == 0, 0.0, halo_ref[...])
    ext = jnp.concatenate([halo, u_ref[...]], axis=0)
    t = i * OUT_ROWS + lax.broadcasted_iota(jnp.int32, (OUT_ROWS, 1), 0)

    out = jnp.dot(ya_ref[...], wo_ref[:D_SB, :], preferred_element_type=jnp.float32)
    win = ext
    width = 1
    for g, w in enumerate(POOL_WINDOWS):
        while width < w:
            win = win + pltpu.roll(win, width, axis=0)
            width *= 2
        cols = slice(g * POOL_GROUP, (g + 1) * POOL_GROUP)
        count = jnp.minimum(t + 1, w).astype(jnp.float32)
        mixed = win[HALO:, cols] / count - ext[HALO:, cols]
        y = jnp.dot(mixed.astype(jnp.bfloat16), wp_ref[g], preferred_element_type=jnp.float32)
        gb = gb_ref[:, cols]
        y = y * ps_ref[:, cols] * (gb * jax.nn.sigmoid(gb))
        out = out + jnp.dot(y.astype(jnp.bfloat16), wo_ref[D_SB + g * POOL_GROUP:D_SB + (g + 1) * POOL_GROUP, :],
                            preferred_element_type=jnp.float32)

    ms = jnp.mean(out * out, axis=-1, keepdims=True)
    o_ref[...] = x_ref[...] + out * lax.rsqrt(ms + EPS) * gain_ref[...]


def _out_stage(ya, gub, wp_bf16, pool_scale, wo_bf16, x2d, gain):
    s = x2d.shape[0]
    halo_blocks = OUT_ROWS // HALO
    return pl.pallas_call(
        _out_kernel,
        grid=(s // OUT_ROWS,),
        in_specs=[
            pl.BlockSpec((OUT_ROWS, D_SB), lambda i: (i, 0)),
            pl.BlockSpec((OUT_ROWS, D_POOL), lambda i: (i, 1)),
            pl.BlockSpec((HALO, D_POOL), lambda i: (jnp.maximum(i * halo_blocks - 1, 0), 1)),
            pl.BlockSpec((OUT_ROWS, D_POOL), lambda i: (i, 2)),
            pl.BlockSpec((len(POOL_WINDOWS), POOL_GROUP, POOL_GROUP), lambda i: (0, 0, 0)),
            pl.BlockSpec((1, D_POOL), lambda i: (0, 0)),
            pl.BlockSpec((D_SB + D_POOL, D_MODEL), lambda i: (0, 0)),
            pl.BlockSpec((OUT_ROWS, D_MODEL), lambda i: (i, 0)),
            pl.BlockSpec((1, D_MODEL), lambda i: (0, 0)),
        ],
        out_specs=pl.BlockSpec((OUT_ROWS, D_MODEL), lambda i: (i, 0)),
        out_shape=jax.ShapeDtypeStruct((s, D_MODEL), jnp.float32),
        compiler_params=pltpu.CompilerParams(
            dimension_semantics=("arbitrary",),
            vmem_limit_bytes=VMEM_LIMIT,
        ),
        name="pool_outproj",
    )(ya, gub, gub, gub, wp_bf16, pool_scale, wo_bf16, x2d, gain)


def kernel(x, pre_norm_gain, w_in, w_pool, pool_scale, w_out, post_norm_gain):
    b, s, d = x.shape
    assert b == 1
    h = x.reshape(b * s, d)
    for layer in range(w_in.shape[0]):
        qkv, gub = _inproj(h, pre_norm_gain[layer][None, :], w_in[layer].astype(jnp.bfloat16))
        ya = _attention(qkv, gub)
        h = _out_stage(ya, gub, w_pool[layer].astype(jnp.bfloat16), pool_scale[layer][None, :],
                       w_out[layer].astype(jnp.bfloat16), h, post_norm_gain[layer][None, :])
    return h.reshape(b, s, d)
```
